```python
import jax, jax.numpy as jnp
from jax import lax
import numpy as np

D_MODEL = 1024
BATCH = 4
SEQ = 4096
DEPTH = 2
DEC_BATCH = 32
DEC_SEQ = 8
PAST_LEN = 8192
PAGE_SIZE = 128

D_MIX = D_MODEL
C_A = D_MIX // 4
CPG = 16
G_A = C_A // CPG
P_STATE = 64
C_B = D_MIX // 4
H_B = 4
HD_B = C_B // H_B
CHUNK = 128
C_C = D_MIX // 2
D_HEAD = 64
H_C = C_C // D_HEAD
Q_BLOCK = 128
N_IN = C_A + 2 * C_B + 3 * C_C
D_FF = 2816
N_EXPERTS = 8
TOP_K = 2
N_DENSE = (DEPTH + 1) // 2
N_MOE = DEPTH // 2
PLE_DIM = 256
SB_BIAS_INIT = -6.0
EPS = 1e-6

kernel_name = "hybrid_s5_sgu_stickbreak_decoder_step"


def rms_norm(x, g):
    xf = x.astype(jnp.float32)
    y = xf * lax.rsqrt(jnp.mean(xf * xf, axis=-1, keepdims=True) + EPS)
    return (y * g.astype(jnp.float32)).astype(x.dtype)


def _cplx_combine(e1, e2):
    a1r, a1i, b1r, b1i = e1
    a2r, a2i, b2r, b2i = e2
    ar = a2r * a1r - a2i * a1i
    ai = a2r * a1i + a2i * a1r
    br = a2r * b1r - a2i * b1i + b2r
    bi = a2r * b1i + a2i * b1r + b2i
    return (ar, ai, br, bi)


def s5_mixer(u, h0_re, h0_im, a_re, a_im, log_dt, b_re, b_im, c_re, c_im, d, w_glu):
    f32 = jnp.float32
    bsz, t, _ = u.shape
    uf = u.astype(f32)
    ug = uf.reshape(bsz, t, G_A, CPG)
    a_re = a_re.astype(f32)
    a_im = a_im.astype(f32)
    dt = jnp.exp(log_dt.astype(f32))[:, None]
    mag = jnp.exp(a_re * dt)
    ab_re = mag * jnp.cos(a_im * dt)
    ab_im = mag * jnp.sin(a_im * dt)
    den = a_re * a_re + a_im * a_im
    num_re = ab_re - 1.0
    coef_re = (num_re * a_re + ab_im * a_im) / den
    coef_im = (ab_im * a_re - num_re * a_im) / den
    bu_re = jnp.einsum('gpc,btgc->btgp', b_re.astype(f32), ug)
    bu_im = jnp.einsum('gpc,btgc->btgp', b_im.astype(f32), ug)
    x_re = coef_re * bu_re - coef_im * bu_im
    x_im = coef_re * bu_im + coef_im * bu_re
    ar = jnp.broadcast_to(ab_re, x_re.shape)
    ai = jnp.broadcast_to(ab_im, x_re.shape)
    cr, ci, hr, hi = lax.associative_scan(_cplx_combine, (ar, ai, x_re, x_im), axis=1)
    h0r = h0_re.astype(f32)[:, None]
    h0i = h0_im.astype(f32)[:, None]
    hr = hr + cr * h0r - ci * h0i
    hi = hi + cr * h0i + ci * h0r
    y = jnp.einsum('gcp,btgp->btgc', c_re.astype(f32), hr) - jnp.einsum('gcp,btgp->btgc', c_im.astype(f32), hi)
    y = y.reshape(bsz, t, C_A) + d.astype(f32) * uf
    y = jax.nn.gelu(y)
    gl = y @ w_glu.astype(f32)
    out = gl[..., :C_A] * jax.nn.sigmoid(gl[..., C_A:])
    return out.astype(u.dtype), hr[:, -1].astype(h0_re.dtype), hi[:, -1].astype(h0_im.dtype)


def sgu_mixer(u, v, g_v, w_s, b_s):
    bsz, t, _ = u.shape
    length = min(t, CHUNK)
    n_chunks = t // length
    vn = rms_norm(v, g_v)
    vh = vn.reshape(bsz, n_chunks, length, H_B, HD_B)
    mask = jnp.tril(jnp.ones((length, length), dtype=w_s.dtype))
    ws = w_s[:, :length, :length] * mask
    bias = jnp.transpose(b_s[:, :length])[None, None, :, :, None]
    mixed = jnp.einsum('hts,bcshd->bcthd', ws, vh) + bias
    y = u * mixed.reshape(bsz, t, C_B)
    return y, vn


def stick_breaking(q, k, v, q_pos, k_pos, sb_bias):
    f32 = jnp.float32
    bsz, t, h, dh = q.shape
    qb = min(t, Q_BLOCK)
    nb = t // qb
    scale = dh ** -0.5
    kf = k.astype(f32)
    vf = v.astype(f32)
    bias = sb_bias.astype(f32)[None, :, None, None]
    qs = jnp.transpose(q.reshape(bsz, nb, qb, h, dh), (1, 0, 2, 3, 4))
    ps = q_pos.reshape(nb, qb)

    def block(args):
        qblk, pblk = args
        z = jnp.einsum('bqhd,bkhd->bhqk', qblk.astype(f32), kf) * scale + bias
        mask = (k_pos[None, :] < pblk[:, None])[None, None]
        log_keep = jnp.where(mask, jax.nn.log_sigmoid(-z), 0.0)
        rest = lax.cumsum(log_keep, axis=3, reverse=True) - log_keep
        a = jnp.where(mask, jnp.exp(jax.nn.log_sigmoid(z) + rest), 0.0)
        return jnp.einsum('bhqk,bkhd->bqhd', a, vf)

    out = lax.map(block, (qs, ps))
    return jnp.transpose(out, (1, 0, 2, 3, 4)).reshape(bsz, t, h, dh).astype(v.dtype)


def swiglu(x, w_up, w_down):
    gu = x @ w_up
    return (jax.nn.silu(gu[..., :D_FF]) * gu[..., D_FF:]) @ w_down


def moe_ffn(x, w_router, b_router, w_up, w_down):
    f32 = jnp.float32
    logits = x.astype(f32) @ w_router.astype(f32) + b_router.astype(f32)
    top_v, top_i = lax.top_k(logits, TOP_K)
    gw = jax.nn.softmax(top_v, axis=-1)
    gates = jnp.einsum('btk,btke->bte', gw, jax.nn.one_hot(top_i, N_EXPERTS, dtype=f32)).astype(x.dtype)
    out = jnp.zeros_like(x)
    for e in range(N_EXPERTS):
        out = out + gates[..., e:e + 1] * swiglu(x, w_up[e], w_down[e])
    return out


def layer(i, h, p, h0_re, h0_im, k_past, v_past, w):
    bsz, t, _ = h.shape
    xn = rms_norm(h, w['g_mix'][i])
    proj = xn @ w['w_in'][i]
    o1 = C_A
    o2 = o1 + C_B
    o3 = o2 + C_B
    o4 = o3 + C_C
    o5 = o4 + C_C
    u_a = proj[..., :o1]
    u_b = proj[..., o1:o2]
    v_b = proj[..., o2:o3]
    q = proj[..., o3:o4].reshape(bsz, t, H_C, D_HEAD)
    k = proj[..., o4:o5].reshape(bsz, t, H_C, D_HEAD)
    v = proj[..., o5:].reshape(bsz, t, H_C, D_HEAD)
    y_a, hT_re, hT_im = s5_mixer(u_a, h0_re, h0_im, w['ssm_a_re'][i], w['ssm_a_im'][i], w['ssm_log_dt'][i],
                                 w['ssm_b_re'][i], w['ssm_b_im'][i], w['ssm_c_re'][i], w['ssm_c_im'][i],
                                 w['ssm_d'][i], w['w_ssm_glu'][i])
    y_b, vn_b = sgu_mixer(u_b, v_b, w['sgu_g_v'][i], w['sgu_w_s'][i], w['sgu_b_s'][i])
    q = rms_norm(q, w['g_q'][i])
    k = rms_norm(k, w['g_k'][i])
    if k_past is None:
        keys, vals = k, v
    else:
        keys = jnp.concatenate([k_past.astype(k.dtype), k], axis=1)
        vals = jnp.concatenate([v_past.astype(v.dtype), v], axis=1)
    n_keys = keys.shape[1]
    k_pos = jnp.arange(n_keys, dtype=jnp.int32)
    q_pos = (n_keys - t) + jnp.arange(t, dtype=jnp.int32)
    y_c = stick_breaking(q, keys, vals, q_pos, k_pos, w['sb_bias'][i]).reshape(bsz, t, C_C)
    g_grp = w['g_grp'][i]
    mix = jnp.concatenate([rms_norm(y_a, g_grp[:C_A]),
                           rms_norm(y_b, g_grp[C_A:C_A + C_B]),
                           rms_norm(y_c, g_grp[C_A + C_B:])], axis=-1)
    h = h + mix @ w['w_out'][i]
    xn = rms_norm(h, w['g_ffn'][i])
    if i % 2 == 0:
        f = swiglu(xn, w['w_ffn_up'][i // 2], w['w_ffn_down'][i // 2])
    else:
        j = i // 2
        f = moe_ffn(xn, w['w_router'][j], w['b_router'][j], w['w_exp_up'][j], w['w_exp_down'][j])
    h = h + f
    gate = jax.nn.sigmoid(rms_norm(h, w['g_ple'][i]) @ w['w_ple_gate'][i])
    h = h + gate * (p.astype(h.dtype) @ w['w_ple'][i])
    return h, hT_re, hT_im, k, v, vn_b


def setup_inputs(seed: int = 0) -> dict:
    key = jax.random.key(seed)
    ks = iter(jax.random.split(key, 48))
    f32 = jnp.float32
    n_pages = PAST_LEN // PAGE_SIZE
    n_pool = (5 * DEC_BATCH * n_pages) // 4

    def nrm(shape, scale):
        return jax.random.normal(next(ks), shape, f32) * scale

    def gain(shape):
        return 1.0 + nrm(shape, 0.05)

    x_prompt = nrm((BATCH, SEQ, D_MODEL), 1.0)
    x_sample = nrm((DEC_BATCH, DEC_SEQ, D_MODEL), 1.0)
    p_prompt = nrm((DEPTH, BATCH, SEQ, PLE_DIM), 1.0)
    p_sample = nrm((DEPTH, DEC_BATCH, DEC_SEQ, PLE_DIM), 1.0)
    state_ssm_re = nrm((DEPTH, DEC_BATCH, G_A, P_STATE), 0.3)
    state_ssm_im = nrm((DEPTH, DEC_BATCH, G_A, P_STATE), 0.3)
    cache_k = nrm((DEPTH, n_pool, PAGE_SIZE, H_C, D_HEAD), 1.0)
    cache_v = nrm((DEPTH, n_pool, PAGE_SIZE, H_C, D_HEAD), 1.0)
    page_table = jax.random.permutation(next(ks), n_pool)[:DEC_BATCH * n_pages].reshape(DEC_BATCH, n_pages).astype(jnp.int32)

    n_idx = jnp.arange(P_STATE, dtype=f32)
    ssm_a_re = -0.5 * jnp.exp(nrm((DEPTH, G_A, P_STATE), 0.02))
    ssm_a_im = jnp.pi * n_idx[None, None, :] + nrm((DEPTH, G_A, P_STATE), 0.02)
    ssm_log_dt = jax.random.uniform(next(ks), (DEPTH, G_A), f32, np.log(1e-3), np.log(1e-1))
    return {
        'x_prompt': x_prompt, 'x_sample': x_sample, 'p_prompt': p_prompt, 'p_sample': p_sample,
        'state_ssm_re': state_ssm_re, 'state_ssm_im': state_ssm_im,
        'cache_k': cache_k, 'cache_v': cache_v, 'page_table': page_table,
        'g_mix': gain((DEPTH, D_MODEL)),
        'w_in': nrm((DEPTH, D_MODEL, N_IN), D_MODEL ** -0.5),
        'ssm_a_re': ssm_a_re, 'ssm_a_im': ssm_a_im, 'ssm_log_dt': ssm_log_dt,
        'ssm_b_re': nrm((DEPTH, G_A, P_STATE, CPG), (2 * CPG) ** -0.5),
        'ssm_b_im': nrm((DEPTH, G_A, P_STATE, CPG), (2 * CPG) ** -0.5),
        'ssm_c_re': nrm((DEPTH, G_A, CPG, P_STATE), (2 * P_STATE) ** -0.5),
        'ssm_c_im': nrm((DEPTH, G_A, CPG, P_STATE), (2 * P_STATE) ** -0.5),
        'ssm_d': nrm((DEPTH, C_A), 1.0),
        'w_ssm_glu': nrm((DEPTH, C_A, 2 * C_A), C_A ** -0.5),
        'sgu_g_v': gain((DEPTH, C_B)),
        'sgu_w_s': nrm((DEPTH, H_B, CHUNK, CHUNK), CHUNK ** -0.5),
        'sgu_b_s': 1.0 + nrm((DEPTH, H_B, CHUNK), 0.1),
        'g_q': gain((DEPTH, D_HEAD)), 'g_k': gain((DEPTH, D_HEAD)),
        'sb_bias': SB_BIAS_INIT + nrm((DEPTH, H_C), 0.1),
        'g_grp': gain((DEPTH, D_MIX)),
        'w_out': nrm((DEPTH, D_MIX, D_MODEL), D_MIX ** -0.5),
        'g_ffn': gain((DEPTH, D_MODEL)),
        'w_ffn_up': nrm((N_DENSE, D_MODEL, 2 * D_FF), D_MODEL ** -0.5),
        'w_ffn_down': nrm((N_DENSE, D_FF, D_MODEL), D_FF ** -0.5),
        'w_router': nrm((N_MOE, D_MODEL, N_EXPERTS), D_MODEL ** -0.5),
        'b_router': nrm((N_MOE, N_EXPERTS), 0.01),
        'w_exp_up': nrm((N_MOE, N_EXPERTS, D_MODEL, 2 * D_FF), D_MODEL ** -0.5),
        'w_exp_down': nrm((N_MOE, N_EXPERTS, D_FF, D_MODEL), D_FF ** -0.5),
        'w_ple': nrm((DEPTH, PLE_DIM, D_MODEL), PLE_DIM ** -0.5),
        'g_ple': gain((DEPTH, D_MODEL)),
        'w_ple_gate': nrm((DEPTH, D_MODEL, D_MODEL), D_MODEL ** -0.5),
    }


def reference(x_prompt, x_sample, p_prompt, p_sample, state_ssm_re, state_ssm_im, cache_k, cache_v, page_table,
              g_mix, w_in, ssm_a_re, ssm_a_im, ssm_log_dt, ssm_b_re, ssm_b_im, ssm_c_re, ssm_c_im, ssm_d, w_ssm_glu,
              sgu_g_v, sgu_w_s, sgu_b_s, g_q, g_k, sb_bias, g_grp, w_out, g_ffn, w_ffn_up, w_ffn_down,
              w_router, b_router, w_exp_up, w_exp_down, w_ple, g_ple, w_ple_gate):
    w = dict(g_mix=g_mix, w_in=w_in, ssm_a_re=ssm_a_re, ssm_a_im=ssm_a_im, ssm_log_dt=ssm_log_dt,
             ssm_b_re=ssm_b_re, ssm_b_im=ssm_b_im, ssm_c_re=ssm_c_re, ssm_c_im=ssm_c_im, ssm_d=ssm_d,
             w_ssm_glu=w_ssm_glu, sgu_g_v=sgu_g_v, sgu_w_s=sgu_w_s, sgu_b_s=sgu_b_s, g_q=g_q, g_k=g_k,
             sb_bias=sb_bias, g_grp=g_grp, w_out=w_out, g_ffn=g_ffn, w_ffn_up=w_ffn_up, w_ffn_down=w_ffn_down,
             w_router=w_router, b_router=b_router, w_exp_up=w_exp_up, w_exp_down=w_exp_down,
             w_ple=w_ple, g_ple=g_ple, w_ple_gate=w_ple_gate)
    n_seq, n_pages = page_table.shape
    hp = x_prompt
    hs = x_sample
    bp = x_prompt.shape[0]
    zeros_state = jnp.zeros((bp, G_A, P_STATE), dtype=state_ssm_re.dtype)
    sre_p, sim_p, sre_s, sim_s = [], [], [], []
    kp_l, vp_l, ks_l, vs_l, vb_l = [], [], [], [], []
    for i in range(DEPTH):
        hp, r_p, m_p, k_p, v_p, _ = layer(i, hp, p_prompt[i], zeros_state, zeros_state, None, None, w)
        k_past = cache_k[i][page_table].reshape(n_seq, n_pages * PAGE_SIZE, H_C, D_HEAD)
        v_past = cache_v[i][page_table].reshape(n_seq, n_pages * PAGE_SIZE, H_C, D_HEAD)
        hs, r_s, m_s, k_s, v_s, vb_s = layer(i, hs, p_sample[i], state_ssm_re[i], state_ssm_im[i],
                                             k_past, v_past, w)
        sre_p.append(r_p)
        sim_p.append(m_p)
        sre_s.append(r_s)
        sim_s.append(m_s)
        kp_l.append(k_p)
        vp_l.append(v_p)
        ks_l.append(k_s)
        vs_l.append(v_s)
        vb_l.append(vb_s)
    ssm_re_prompt = jnp.stack(sre_p)
    ssm_im_prompt = jnp.stack(sim_p)
    ssm_re_sample = jnp.stack(sre_s)
    ssm_im_sample = jnp.stack(sim_s)
    k_prompt = jnp.stack(kp_l)
    v_prompt = jnp.stack(vp_l)
    k_sample = jnp.stack(ks_l)
    v_sample = jnp.stack(vs_l)
    sgu_v_sample = jnp.stack(vb_l)
    return (hp, hs, ssm_re_prompt, ssm_im_prompt, ssm_re_sample, ssm_im_sample,
            k_prompt, v_prompt, k_sample, v_sample, sgu_v_sample)
```

```python
import functools

import jax
import jax.numpy as jnp
from jax import lax
from jax.experimental import pallas as pl
from jax.experimental.pallas import tpu as pltpu

F32 = jnp.float32
BF16 = jnp.bfloat16
EPS = 1e-6

C_A = 256
CPG = 16
G_A = 16
P_STATE = 64
N_STATE = G_A * P_STATE
C_B = 256
H_B = 4
HD_B = 64
CHUNK = 128
C_C = 512
D_HEAD = 64
H_C = 8
D_FF = 2816
N_EXPERTS = 8
PAGE = 128
LANES = 128

VMEM_LIMIT = 56 * 1024 * 1024


def _params(*sem):
    return pltpu.CompilerParams(dimension_semantics=sem, vmem_limit_bytes=VMEM_LIMIT)


def _dot(a, b):
    if b.dtype == BF16 or a.dtype == BF16:
        return jnp.dot(a.astype(BF16), b.astype(BF16), preferred_element_type=F32)
    return jnp.dot(a, b, preferred_element_type=F32, precision=lax.Precision.HIGHEST)


def _dot_nt(a, b):
    dn = (((1,), (1,)), ((), ()))
    if b.dtype == BF16 or a.dtype == BF16:
        return lax.dot_general(a.astype(BF16), b.astype(BF16), dn, preferred_element_type=F32)
    return lax.dot_general(a, b, dn, preferred_element_type=F32, precision=lax.Precision.HIGHEST)


def _dot_split(a, b_bf16):
    hi = a.astype(BF16)
    lo = (a - hi.astype(F32)).astype(BF16)
    return (jnp.dot(hi, b_bf16, preferred_element_type=F32)
            + jnp.dot(lo, b_bf16, preferred_element_type=F32))


def _rms(x, g):
    return x * lax.rsqrt(jnp.mean(x * x, axis=-1, keepdims=True) + EPS) * g


def _sigmoid(x):
    return 1.0 / (1.0 + jnp.exp(-x))


def _softplus(z):
    return jnp.maximum(z, 0.0) + jnp.log(1.0 + jnp.exp(-jnp.abs(z)))


def _inproj_kernel(h_ref, g_ref, w_ref, gq_ref, gk_ref, hm_ref, *out_refs, kv_copies):
    ua_ref, ub_ref, vb_ref, qa_ref, k_ref, v_ref = out_refs[:6]
    x = h_ref[...]
    xn = _rms(x, g_ref[...])
    proj = _dot(xn, w_ref[...])
    o1, o2, o3 = C_A, C_A + C_B, C_A + 2 * C_B
    o4, o5 = o3 + C_C, o3 + 2 * C_C
    ua_ref[...] = proj[:, :o1]
    ub_ref[...] = proj[:, o1:o2]
    vb_ref[...] = proj[:, o2:o3]
    q = proj[:, o3:o4]
    k = proj[:, o4:o5]
    v = proj[:, o5:]
    hm = hm_ref[...]
    qn = q * lax.rsqrt(_dot_split(q * q, hm) + EPS) * gq_ref[...]
    kn = k * lax.rsqrt(_dot_split(k * k, hm) + EPS) * gk_ref[...]
    qa_ref[...] = (qn * (D_HEAD ** -0.5)).astype(qa_ref.dtype)
    k_ref[...] = kn
    v_ref[...] = v
    if kv_copies:
        ka_ref, va_ref = out_refs[6:]
        ka_ref[...] = kn.astype(ka_ref.dtype)
        va_ref[...] = v.astype(va_ref.dtype)


def _inproj(h, g_mix, w_in, g_q, g_k, head_mean, *, tm, att_dtype, kv_copies):
    n, d = h.shape
    n_in = w_in.shape[1]
    row = lambda w: pl.BlockSpec((tm, w), lambda i: (i, 0))
    full = lambda a: pl.BlockSpec(a.shape, lambda i: (0,) * a.ndim)
    gq = jnp.tile(g_q.astype(F32), H_C)[None, :]
    gk = jnp.tile(g_k.astype(F32), H_C)[None, :]
    gm = g_mix.astype(F32)[None, :]
    out_shape = [jax.ShapeDtypeStruct((n, C_A), F32), jax.ShapeDtypeStruct((n, C_B), F32),
                 jax.ShapeDtypeStruct((n, C_B), F32), jax.ShapeDtypeStruct((n, C_C), att_dtype),
                 jax.ShapeDtypeStruct((n, C_C), F32), jax.ShapeDtypeStruct((n, C_C), F32)]
    out_specs = [row(C_A), row(C_B), row(C_B), row(C_C), row(C_C), row(C_C)]
    if kv_copies:
        out_shape += [jax.ShapeDtypeStruct((n, C_C), att_dtype)] * 2
        out_specs += [row(C_C), row(C_C)]
    return pl.pallas_call(
        functools.partial(_inproj_kernel, kv_copies=kv_copies),
        grid=(n // tm,),
        in_specs=[row(d), full(gm), full(w_in), full(gq), full(gk), full(head_mean)],
        out_specs=out_specs,
        out_shape=out_shape,
        compiler_params=_params("parallel"),
        name="inproj",
    )(h, gm, w_in, gq, gk, head_mean)


def _s5_kernel(u_ref, bbar_ref, ab_ref, cmat_ref, d_ref, wglu_ref, h0_ref, y_ref, ht_ref,
               x_s, h_s, p_s, c_s, *, seg_len, n_groups, chain):
    n = N_STATE
    s = seg_len
    nc = n // LANES
    u = u_ref[...]
    x = _dot(u, bbar_ref[...])
    for c in range(2 * nc):
        x_s[c] = x[:, c * LANES:(c + 1) * LANES]
    ar = jnp.broadcast_to(ab_ref[0:1, :], (8, n))
    ai = jnp.broadcast_to(ab_ref[1:2, :], (8, n))

    for g in range(n_groups):
        base = g * 8 * s
        if chain:
            hr0 = jnp.zeros((8, n), F32)
            hi0 = jnp.zeros((8, n), F32)
        else:
            hr0 = h0_ref[g * 8:(g + 1) * 8, :n]
            hi0 = h0_ref[g * 8:(g + 1) * 8, n:]

        def step(t, carry, base=base):
            hr, hi = carry
            rows = pl.ds(base + t, 8, stride=s)
            xr = jnp.concatenate([x_s[c, rows, :] for c in range(nc)], axis=1)
            xi = jnp.concatenate([x_s[nc + c, rows, :] for c in range(nc)], axis=1)
            nr = ar * hr - ai * hi + xr
            ni = ar * hi + ai * hr + xi
            for c in range(nc):
                h_s[c, rows, :] = nr[:, c * LANES:(c + 1) * LANES]
                h_s[nc + c, rows, :] = ni[:, c * LANES:(c + 1) * LANES]
            return nr, ni

        hr, hi = lax.fori_loop(0, s, step, (hr0, hi0))
        if not chain:
            ht_ref[g * 8:(g + 1) * 8, :n] = hr
            ht_ref[g * 8:(g + 1) * 8, n:] = hi

    if chain:
        first = pl.program_id(1) == 0

        @pl.when(first)
        def _():
            p_s[0:1, :n] = ab_ref[0:1, :]
            p_s[0:1, n:] = ab_ref[1:2, :]
            k = 1
            while k < s:
                lr = p_s[k - 1:k, :n]
                li = p_s[k - 1:k, n:]
                pr = p_s[0:k, :n]
                pi = p_s[0:k, n:]
                p_s[k:2 * k, :n] = pr * lr - pi * li
                p_s[k:2 * k, n:] = pr * li + pi * lr
                k *= 2
            c_s[...] = h0_ref[...]

        for c in range(nc):
            lanes = slice(c * LANES, (c + 1) * LANES)
            pr = p_s[:, lanes]
            pi = p_s[:, n + c * LANES:n + (c + 1) * LANES]
            cr = c_s[:, lanes]
            ci = c_s[:, n + c * LANES:n + (c + 1) * LANES]
            for j in range(8):
                rows = slice(j * s, (j + 1) * s)
                hr = h_s[c, rows, :] + pr * cr - pi * ci
                hi = h_s[nc + c, rows, :] + pr * ci + pi * cr
                h_s[c, rows, :] = hr
                h_s[nc + c, rows, :] = hi
                cr = hr[s - 1:s]
                ci = hi[s - 1:s]
            c_s[:, lanes] = cr
            c_s[:, n + c * LANES:n + (c + 1) * LANES] = ci
        ht_ref[...] = c_s[...]

    h_all = jnp.concatenate([h_s[c] for c in range(2 * nc)], axis=1)
    y = _dot(h_all, cmat_ref[...]) + d_ref[...] * u
    y = jax.nn.gelu(y)
    gl = _dot(y, wglu_ref[...])
    y_ref[...] = gl[:, :C_A] * _sigmoid(gl[:, C_A:])


def _s5(u, bbar, ab, cmat, d, wglu, h0, *, chain, rows):
    b, t, _ = u.shape
    n2 = 2 * N_STATE
    full = lambda a: pl.BlockSpec(a.shape, lambda *_: (0,) * a.ndim)
    d2 = d.astype(F32)[None, :]
    if chain:
        seg_len = rows // 8
        grid = (b, t // rows)
        u_in = u
        u_spec = pl.BlockSpec((None, rows, C_A), lambda i, j: (i, j, 0))
        y_spec = pl.BlockSpec((None, rows, C_A), lambda i, j: (i, j, 0))
        h0_in = h0[:, None, :]
        h0_spec = pl.BlockSpec((None, 1, n2), lambda i, j: (i, 0, 0))
        ht_spec = pl.BlockSpec((None, 1, n2), lambda i, j: (i, 0, 0))
        ht_shape = jax.ShapeDtypeStruct((b, 1, n2), F32)
        y_shape = jax.ShapeDtypeStruct((b, t, C_A), F32)
        n_groups = 1
        sem = ("parallel", "arbitrary")
    else:
        assert b * t == rows and b % 8 == 0
        seg_len = t
        grid = (1,)
        u_in = u.reshape(rows, C_A)
        u_spec = pl.BlockSpec((rows, C_A), lambda i: (0, 0))
        y_spec = pl.BlockSpec((rows, C_A), lambda i: (0, 0))
        h0_in = h0
        h0_spec = pl.BlockSpec((b, n2), lambda i: (0, 0))
        ht_spec = pl.BlockSpec((b, n2), lambda i: (0, 0))
        ht_shape = jax.ShapeDtypeStruct((b, n2), F32)
        y_shape = jax.ShapeDtypeStruct((rows, C_A), F32)
        n_groups = b // 8
        sem = ("arbitrary",)
    y, ht = pl.pallas_call(
        functools.partial(_s5_kernel, seg_len=seg_len, n_groups=n_groups, chain=chain),
        grid=grid,
        in_specs=[u_spec, full(bbar), full(ab), full(cmat), full(d2), full(wglu), h0_spec],
        out_specs=[y_spec, ht_spec],
        out_shape=[y_shape, ht_shape],
        scratch_shapes=[pltpu.VMEM((n2 // LANES, rows, LANES), F32), pltpu.VMEM((n2 // LANES, rows, LANES), F32),
                        pltpu.VMEM((seg_len, n2), F32), pltpu.VMEM((1, n2), F32)],
        compiler_params=_params(*sem),
        name="s5_mixer",
    )(u_in, bbar, ab, cmat, d2, wglu, h0_in)
    return y.reshape(b * t, C_A), ht.reshape(b, n2)


def _s5_params(a_re, a_im, log_dt, b_re, b_im, c_re, c_im, dtype):
    a_re = a_re.astype(F32)
    a_im = a_im.astype(F32)
    dt = jnp.exp(log_dt.astype(F32))[:, None]
    mag = jnp.exp(a_re * dt)
    ab_re = mag * jnp.cos(a_im * dt)
    ab_im = mag * jnp.sin(a_im * dt)
    den = a_re * a_re + a_im * a_im
    num_re = ab_re - 1.0
    coef_re = (num_re * a_re + ab_im * a_im) / den
    coef_im = (ab_im * a_re - num_re * a_im) / den
    b_re = b_re.astype(F32)
    b_im = b_im.astype(F32)
    bb_re = coef_re[..., None] * b_re - coef_im[..., None] * b_im
    bb_im = coef_re[..., None] * b_im + coef_im[..., None] * b_re
    eye = jnp.eye(G_A, dtype=F32)
    bd_in = lambda m: jnp.einsum('gpc,gh->gchp', m, eye).reshape(C_A, N_STATE)
    bd_out = lambda m: jnp.einsum('gcp,gh->gphc', m.astype(F32), eye).reshape(N_STATE, C_A)
    bbar = jnp.concatenate([bd_in(bb_re), bd_in(bb_im)], axis=1).astype(dtype)
    cmat = jnp.concatenate([bd_out(c_re), -bd_out(c_im)], axis=0).astype(dtype)
    ab = jnp.stack([ab_re.reshape(N_STATE), ab_im.reshape(N_STATE)])
    return bbar, ab, cmat


def _sgu_kernel(u_ref, v_ref, g_ref, w_ref, b_ref, y_ref, *vn_out, length, n_chunks):
    vn = _rms(v_ref[...], g_ref[...])
    if vn_out:
        vn_out[0][...] = vn
    head = lax.broadcasted_iota(jnp.int32, (length, C_B), 1) // HD_B
    bias = b_ref[...]
    for c in range(n_chunks):
        rows = slice(c * length, (c + 1) * length)
        vc = vn[rows]
        mixed = jnp.zeros((length, C_B), F32)
        for hh in range(H_B):
            mixed = jnp.where(head == hh, _dot(w_ref[hh], vc), mixed)
        y_ref[rows, :] = u_ref[rows, :] * (mixed + bias)


def _sgu(u, v, g_v, w_mix, bias, *, tm, want_vn):
    n = u.shape[0]
    length = w_mix.shape[1]
    row = pl.BlockSpec((tm, C_B), lambda i: (i, 0))
    full = lambda a: pl.BlockSpec(a.shape, lambda i: (0,) * a.ndim)
    g2 = g_v.astype(F32)[None, :]
    out_shape = [jax.ShapeDtypeStruct((n, C_B), F32)]
    out_specs = [row]
    if want_vn:
        out_shape.append(jax.ShapeDtypeStruct((n, C_B), F32))
        out_specs.append(row)
    res = pl.pallas_call(
        functools.partial(_sgu_kernel, length=length, n_chunks=tm // length),
        grid=(n // tm,),
        in_specs=[row, row, full(g2), full(w_mix), full(bias)],
        out_specs=out_specs,
        out_shape=out_shape,
        compiler_params=_params("parallel"),
        name="sgu_mixer",
    )(u, v, g2, w_mix, bias)
    return res if want_vn else (res[0], None)


def _attn_prompt_kernel(bias_ref, q_ref, k_ref, v_ref, tri_ref, o_ref, *, tq):
    pair = pl.program_id(1)
    qi = pl.program_id(2)
    q = q_ref[...]
    tri = tri_ref[...]
    lane_q = lax.broadcasted_iota(jnp.int32, (tq, 2 * D_HEAD), 1)
    row = lax.broadcasted_iota(jnp.int32, (tq, tq), 0)
    col = lax.broadcasted_iota(jnp.int32, (tq, tq), 1)
    causal = col < row
    out = jnp.zeros((tq, 2 * D_HEAD), F32)
    for hh in range(2):
        bias = bias_ref[2 * pair + hh]
        qm = jnp.where((lane_q // D_HEAD) == hh, q, jnp.zeros_like(q))

        def block(kb, carry, acc, masked, qm=qm, bias=bias):
            start = pl.multiple_of(kb * tq, tq)
            kblk = k_ref[pl.ds(start, tq), :]
            vblk = v_ref[pl.ds(start, tq), :]
            z = _dot_nt(qm, kblk) + bias
            sp = _softplus(z)
            if masked:
                sp = jnp.where(causal, sp, 0.0)
            later = _dot(sp.astype(BF16), tri)
            a = jnp.exp(z - sp - later - carry)
            if masked:
                a = jnp.where(causal, a, 0.0)
            acc = acc + _dot(a.astype(BF16), vblk)
            carry = carry + jnp.sum(sp, axis=-1, keepdims=True)
            return carry, acc

        carry, acc = block(qi, jnp.zeros((tq, 1), F32), jnp.zeros((tq, 2 * D_HEAD), F32), True)

        def body(i, ca):
            return block(qi - 1 - i, ca[0], ca[1], False)

        carry, acc = lax.fori_loop(0, qi, body, (carry, acc))
        out = jnp.where((lane_q // D_HEAD) == hh, acc, out)
    o_ref[...] = out


def _attn_prompt(q, k, v, sb_bias, *, tq):
    b, t, _ = q.shape
    tri = (jnp.arange(tq)[:, None] > jnp.arange(tq)[None, :]).astype(BF16)
    lanes = 2 * D_HEAD
    return pl.pallas_call(
        functools.partial(_attn_prompt_kernel, tq=tq),
        grid_spec=pltpu.PrefetchScalarGridSpec(
            num_scalar_prefetch=1,
            grid=(b, H_C // 2, t // tq),
            in_specs=[pl.BlockSpec((None, tq, lanes), lambda i, p, j, *_: (i, j, p)),
                      pl.BlockSpec((None, t, lanes), lambda i, p, j, *_: (i, 0, p)),
                      pl.BlockSpec((None, t, lanes), lambda i, p, j, *_: (i, 0, p)),
                      pl.BlockSpec((tq, tq), lambda i, p, j, *_: (0, 0))],
            out_specs=pl.BlockSpec((None, tq, lanes), lambda i, p, j, *_: (i, j, p)),
        ),
        out_shape=jax.ShapeDtypeStruct((b, t, C_C), F32),
        compiler_params=_params("parallel", "parallel", "arbitrary"),
        name="attn_prompt",
    )(sb_bias.astype(F32), q, k, v, tri)


def _attn_sample_kernel(pt_ref, q_ref, kn_ref, vn_ref, bias_ref, *rest, n_slots):
    k_refs = rest[:n_slots]
    v_refs = rest[n_slots:2 * n_slots]
    o_ref = rest[2 * n_slots]
    acc_s, carry_s = rest[2 * n_slots + 1:]
    step = pl.program_id(1)
    q = q_ref[...]
    nq = q.shape[0] // H_C

    def process(kflat, vflat, masked, carry):
        n = kflat.shape[0]
        s_big = _dot_nt(q, kflat.astype(q.dtype))
        lane = lax.broadcasted_iota(jnp.int32, (nq, n), 1)
        lane_head = lane % H_C
        z = jnp.zeros((nq, n), F32)
        for hh in range(H_C):
            z = jnp.where(lane_head == hh, s_big[hh * nq:(hh + 1) * nq, :], z)
        z = z + bias_ref[:, :n]
        sp = _softplus(z)
        if masked:
            qrow = lax.broadcasted_iota(jnp.int32, (nq, n), 0)
            keep = (lane // H_C) < qrow
            sp = jnp.where(keep, sp, 0.0)
        incl = sp
        total = sp
        sh = H_C
        while sh < n:
            shifted = pltpu.roll(incl, n - sh, 1)
            incl = incl + jnp.where(lane < n - sh, shifted, 0.0)
            total = total + pltpu.roll(total, sh, 1)
            sh *= 2
        reps = n // carry.shape[1]
        carry_n = jnp.concatenate([carry] * reps, axis=1) if reps > 1 else carry
        a = jnp.exp(z - incl - carry_n)
        if masked:
            a = jnp.where(keep, a, 0.0)
        a_big = jnp.concatenate(
            [jnp.where(lane_head == hh, a, 0.0) for hh in range(H_C)], axis=0)
        contrib = _dot(a_big.astype(q.dtype), vflat.astype(q.dtype))
        return contrib, carry + total[:, :carry.shape[1]]

    @pl.when(step == 0)
    def _():
        contrib, carry = process(kn_ref[...], vn_ref[...], True, jnp.zeros(carry_s.shape, F32))
        acc_s[...] = contrib
        carry_s[...] = carry

    acc = acc_s[...]
    carry = carry_s[...]
    for r in range(n_slots):
        kflat = k_refs[r][...].reshape(PAGE * H_C, D_HEAD)
        vflat = v_refs[r][...].reshape(PAGE * H_C, D_HEAD)
        contrib, carry = process(kflat, vflat, False, carry)
        acc = acc + contrib
    acc_s[...] = acc
    carry_s[...] = carry

    @pl.when(step == pl.num_programs(1) - 1)
    def _():
        o_ref[...] = acc


def _attn_sample(q_all, k_new, v_new, cache_k, cache_v, page_table, bias_t, layer, *, n_slots):
    n_seq, n_pages = page_table.shape
    steps = n_pages // n_slots

    def page_spec(r):
        def imap(b, s, pt):
            return (layer, pt[b * n_pages + (n_pages - 1 - (s * n_slots + r))], 0, 0, 0)
        return pl.BlockSpec((None, None, PAGE, H_C, D_HEAD), imap)

    seq = lambda a: pl.BlockSpec((None,) + a.shape[1:], lambda b, s, pt: (b,) + (0,) * (a.ndim - 1))
    return pl.pallas_call(
        functools.partial(_attn_sample_kernel, n_slots=n_slots),
        grid_spec=pltpu.PrefetchScalarGridSpec(
            num_scalar_prefetch=1,
            grid=(n_seq, steps),
            in_specs=[seq(q_all), seq(k_new), seq(v_new),
                      pl.BlockSpec(bias_t.shape, lambda b, s, pt: (0, 0))]
                     + [page_spec(r) for r in range(n_slots)]
                     + [page_spec(r) for r in range(n_slots)],
            out_specs=pl.BlockSpec((None,) + q_all.shape[1:], lambda b, s, pt: (b, 0, 0)),
            scratch_shapes=[pltpu.VMEM(q_all.shape[1:], F32), pltpu.VMEM((q_all.shape[1] // H_C, 128), F32)],
        ),
        out_shape=jax.ShapeDtypeStruct(q_all.shape, F32),
        compiler_params=_params("parallel", "arbitrary"),
        name="attn_sample",
    )(page_table.reshape(-1), q_all, k_new, v_new, bias_t,
      *([cache_k] * n_slots), *([cache_v] * n_slots))


def _outproj_kernel(ya_ref, yb_ref, yc_ref, h_ref, g_ref, w_ref, *rest, router):
    g = g_ref[...]
    mix = jnp.concatenate([_rms(ya_ref[...], g[:, :C_A]),
                           _rms(yb_ref[...], g[:, C_A:C_A + C_B]),
                           _rms(yc_ref[...], g[:, C_A + C_B:])], axis=-1)
    h1 = h_ref[...] + _dot(mix, w_ref[...])
    if not router:
        rest[0][...] = h1
        return
    gf_ref, wr_ref, br_ref, o_ref, gates_ref = rest
    o_ref[...] = h1
    xn = _rms(h1, gf_ref[...])
    logits = jnp.dot(xn, wr_ref[...], preferred_element_type=F32,
                     precision=lax.Precision.HIGHEST) + br_ref[...]
    e_idx = lax.broadcasted_iota(jnp.int32, logits.shape, 1).astype(F32)
    none = float(N_EXPERTS)
    v1 = jnp.max(logits, axis=-1, keepdims=True)
    i1 = jnp.min(jnp.where(logits == v1, e_idx, none), axis=-1, keepdims=True)
    rest_l = jnp.where(e_idx == i1, -jnp.inf, logits)
    v2 = jnp.max(rest_l, axis=-1, keepdims=True)
    i2 = jnp.min(jnp.where(rest_l == v2, e_idx, none), axis=-1, keepdims=True)
    e2 = jnp.exp(v2 - v1)
    den = 1.0 + e2
    gates_ref[...] = jnp.where(e_idx == i1, 1.0 / den, 0.0) + jnp.where(e_idx == i2, e2 / den, 0.0)


def _outproj(ya, yb, yc, h, g_grp, w_out, *, tm, router=None):
    n, d = h.shape
    row = lambda w: pl.BlockSpec((tm, w), lambda i: (i, 0))
    full = lambda a: pl.BlockSpec(a.shape, lambda i: (0,) * a.ndim)
    g2 = g_grp.astype(F32)[None, :]
    args = [ya, yb, yc, h, g2, w_out]
    in_specs = [row(C_A), row(C_B), row(C_C), row(d), full(g2), full(w_out)]
    out_shape = [jax.ShapeDtypeStruct((n, d), F32)]
    out_specs = [row(d)]
    if router is not None:
        g_ffn, w_router, b_router = router
        extra = [g_ffn.astype(F32)[None, :], w_router.astype(F32), b_router.astype(F32)[None, :]]
        args += extra
        in_specs += [full(a) for a in extra]
        out_shape.append(jax.ShapeDtypeStruct((n, N_EXPERTS), F32))
        out_specs.append(row(N_EXPERTS))
    res = pl.pallas_call(
        functools.partial(_outproj_kernel, router=router is not None),
        grid=(n // tm,),
        in_specs=in_specs,
        out_specs=out_specs,
        out_shape=out_shape,
        compiler_params=_params("parallel"),
        name="outproj",
    )(*args)
    return res if router is not None else (res[0], None)


def _ffn_kernel(h_ref, g_ref, wg_ref, wu_ref, wd_ref, *rest, gated):
    if gated:
        gates_ref, o_ref, xn_s, acc_s, exp_s = rest
    else:
        o_ref, xn_s, acc_s = rest
        exp_s = acc_s
    e = pl.program_id(1)
    j = pl.program_id(2)
    nj = pl.num_programs(2)

    @pl.when((e == 0) & (j == 0))
    def _():
        h = h_ref[...]
        xn_s[...] = _rms(h, g_ref[...]).astype(xn_s.dtype)
        acc_s[...] = h

    xn = xn_s[...]
    gate = _dot(xn, wg_ref[...])
    up = _dot(xn, wu_ref[...])
    act = gate * _sigmoid(gate) * up
    part = _dot(act, wd_ref[...])
    if gated:
        @pl.when(j == 0)
        def _():
            exp_s[...] = part

        @pl.when(j > 0)
        def _():
            exp_s[...] += part

        @pl.when(j == nj - 1)
        def _():
            gates = gates_ref[...]
            lane = lax.broadcasted_iota(jnp.int32, gates.shape, 1)
            ge = jnp.sum(jnp.where(lane == e, gates, 0.0), axis=-1, keepdims=True)
            acc_s[...] += ge * exp_s[...]
    else:
        acc_s[...] += part

    @pl.when((e == pl.num_programs(1) - 1) & (j == nj - 1))
    def _():
        o_ref[...] = acc_s[...]


def _ffn(h, g_ffn, w_up, w_down, gates=None, *, tm, tf):
    n, d = h.shape
    n_e = w_up.shape[0]
    nj = D_FF // tf
    gated = gates is not None
    g2 = g_ffn.astype(F32)[None, :]
    row = pl.BlockSpec((tm, d), lambda i, e, j: (i, 0))
    in_specs = [row, pl.BlockSpec(g2.shape, lambda i, e, j: (0, 0)),
                pl.BlockSpec((None, d, tf), lambda i, e, j: (e, 0, j)),
                pl.BlockSpec((None, d, tf), lambda i, e, j: (e, 0, j + nj)),
                pl.BlockSpec((None, tf, d), lambda i, e, j: (e, j, 0))]
    args = [h, g2, w_up, w_up, w_down]
    cdt = w_up.dtype
    scratch = [pltpu.VMEM((tm, d), cdt), pltpu.VMEM((tm, d), F32)]
    if gated:
        in_specs.append(pl.BlockSpec((tm, N_EXPERTS), lambda i, e, j: (i, 0)))
        args.append(gates)
        scratch.append(pltpu.VMEM((tm, d), F32))
    return pl.pallas_call(
        functools.partial(_ffn_kernel, gated=gated),
        grid=(n // tm, n_e, nj),
        in_specs=in_specs,
        out_specs=row,
        out_shape=jax.ShapeDtypeStruct((n, d), F32),
        scratch_shapes=scratch,
        compiler_params=_params("parallel", "arbitrary", "arbitrary"),
        name="ffn",
    )(*args)


def _ple_kernel(h_ref, p_ref, g_ref, wg_ref, wp_ref, o_ref):
    h = h_ref[...]
    gate = _sigmoid(_dot(_rms(h, g_ref[...]), wg_ref[...]))
    o_ref[...] = h + gate * _dot(p_ref[...], wp_ref[...])


def _ple(h, p, g_ple, w_gate, w_ple, *, tm):
    n, d = h.shape
    g2 = g_ple.astype(F32)[None, :]
    row = lambda w: pl.BlockSpec((tm, w), lambda i: (i, 0))
    full = lambda a: pl.BlockSpec(a.shape, lambda i: (0,) * a.ndim)
    return pl.pallas_call(
        _ple_kernel,
        grid=(n // tm,),
        in_specs=[row(d), row(p.shape[1]), full(g2), full(w_gate), full(w_ple)],
        out_specs=row(d),
        out_shape=jax.ShapeDtypeStruct((n, d), F32),
        compiler_params=_params("parallel"),
        name="ple",
    )(h, p, g2, w_gate, w_ple)


def _tile(n, want):
    t = min(n, want)
    while n % t:
        t //= 2
    return t


def _layer(i, h, p, w, dtype, *, batch, seq, sample):
    n = batch * seq
    cast = lambda a: a.astype(dtype)
    head_mean = (jnp.arange(C_C)[:, None] // D_HEAD == jnp.arange(C_C)[None, :] // D_HEAD)
    head_mean = (head_mean.astype(F32) / D_HEAD).astype(BF16)
    att_dtype = dtype
    ua, ub, vb, qa, k, v, *kv_att = _inproj(
        h, w['g_mix'][i], cast(w['w_in'][i]), w['g_q'][i], w['g_k'][i], head_mean,
        tm=_tile(n, 512), att_dtype=att_dtype, kv_copies=sample is None)

    bbar, ab, cmat = _s5_params(w['ssm_a_re'][i], w['ssm_a_im'][i], w['ssm_log_dt'][i],
                                w['ssm_b_re'][i], w['ssm_b_im'][i], w['ssm_c_re'][i], w['ssm_c_im'][i], dtype)
    wglu = cast(w['w_ssm_glu'][i])
    w_s = w['sgu_w_s'][i]
    b_s = w['sgu_b_s'][i]
    if sample is None:
        h0 = jnp.zeros((batch, 2 * N_STATE), F32)
        ya, ht = _s5(ua.reshape(batch, seq, C_A), bbar, ab, cmat, w['ssm_d'][i], wglu, h0,
                     chain=True, rows=_tile(seq, 512))
        length = min(seq, CHUNK)
        w_mix = cast(w_s[:, :length, :length] * jnp.tril(jnp.ones((length, length), w_s.dtype)))
        bias = jnp.repeat(jnp.transpose(b_s[:, :length]), HD_B, axis=1).astype(F32)
        yb, vn = _sgu(ub, vb, w['sgu_g_v'][i], w_mix, bias, tm=_tile(n, 512), want_vn=False)
        yc = _attn_prompt(qa.reshape(batch, seq, C_C), kv_att[0].reshape(batch, seq, C_C),
                          kv_att[1].reshape(batch, seq, C_C), w['sb_bias'][i], tq=_tile(seq, 256))
        yc = yc.reshape(n, C_C)
    else:
        ya, ht = _s5(ua.reshape(batch, seq, C_A), bbar, ab, cmat, w['ssm_d'][i], wglu, sample['h0'],
                     chain=False, rows=n)
        length = min(seq, CHUNK)
        blk = w_s[:, :length, :length] * jnp.tril(jnp.ones((length, length), w_s.dtype))
        w_mix = cast(jnp.einsum('hts,bc->hbtcs', blk, jnp.eye(batch, dtype=blk.dtype)).reshape(H_B, n, n))
        bias = jnp.tile(jnp.repeat(jnp.transpose(b_s[:, :length]), HD_B, axis=1), (batch, 1)).astype(F32)
        yb, vn = _sgu(ub, vb, w['sgu_g_v'][i], w_mix, bias, tm=n, want_vn=True)
        q_all = jnp.transpose(qa.reshape(batch, seq, H_C, D_HEAD), (0, 2, 1, 3)).reshape(batch, H_C * seq, D_HEAD)
        pad = 128 // H_C - seq
        k_new = jnp.pad(k.reshape(batch, seq * H_C, D_HEAD), ((0, 0), (0, pad * H_C), (0, 0)))
        v_new = jnp.pad(v.reshape(batch, seq * H_C, D_HEAD), ((0, 0), (0, pad * H_C), (0, 0)))
        bias_t = jnp.tile(w['sb_bias'][i].astype(F32), (seq, PAGE))
        yc = _attn_sample(q_all, k_new, v_new, sample['cache_k'], sample['cache_v'], sample['page_table'],
                          bias_t, i, n_slots=sample['n_slots'])
        yc = jnp.transpose(yc.reshape(batch, H_C, seq, D_HEAD), (0, 2, 1, 3)).reshape(n, C_C)

    moe = i % 2 == 1
    jj = i // 2
    router = (w['g_ffn'][i], w['w_router'][jj], w['b_router'][jj]) if moe else None
    h1, gates = _outproj(ya, yb, yc, h, w['g_grp'][i], cast(w['w_out'][i]), tm=_tile(n, 512), router=router)
    if moe:
        h2 = _ffn(h1, w['g_ffn'][i], cast(w['w_exp_up'][jj]), cast(w['w_exp_down'][jj]), gates,
                  tm=_tile(n, 1024), tf=256)
    else:
        h2 = _ffn(h1, w['g_ffn'][i], cast(w['w_ffn_up'][jj])[None], cast(w['w_ffn_down'][jj])[None],
                  tm=_tile(n, 1024), tf=256)
    h3 = _ple(h2, p, w['g_ple'][i], cast(w['w_ple_gate'][i]), cast(w['w_ple'][i]), tm=_tile(n, 512))
    return h3, ht, k, v, vn


def kernel(x_prompt, x_sample, p_prompt, p_sample, state_ssm_re, state_ssm_im, cache_k, cache_v, page_table, g_mix, w_in, ssm_a_re, ssm_a_im, ssm_log_dt, ssm_b_re, ssm_b_im, ssm_c_re, ssm_c_im, ssm_d, w_ssm_glu, sgu_g_v, sgu_w_s, sgu_b_s, g_q, g_k, sb_bias, g_grp, w_out, g_ffn, w_ffn_up, w_ffn_down, w_router, b_router, w_exp_up, w_exp_down, w_ple, g_ple, w_ple_gate):
    w = dict(g_mix=g_mix, w_in=w_in, ssm_a_re=ssm_a_re, ssm_a_im=ssm_a_im, ssm_log_dt=ssm_log_dt,
             ssm_b_re=ssm_b_re, ssm_b_im=ssm_b_im, ssm_c_re=ssm_c_re, ssm_c_im=ssm_c_im, ssm_d=ssm_d,
             w_ssm_glu=w_ssm_glu, sgu_g_v=sgu_g_v, sgu_w_s=sgu_w_s, sgu_b_s=sgu_b_s, g_q=g_q, g_k=g_k,
             sb_bias=sb_bias, g_grp=g_grp, w_out=w_out, g_ffn=g_ffn, w_ffn_up=w_ffn_up, w_ffn_down=w_ffn_down,
             w_router=w_router, b_router=b_router, w_exp_up=w_exp_up, w_exp_down=w_exp_down,
             w_ple=w_ple, g_ple=g_ple, w_ple_gate=w_ple_gate)
    depth = w_in.shape[0]
    bp, tp, d = x_prompt.shape
    bs, ts, _ = x_sample.shape
    n_pages = page_table.shape[1]
    hp = x_prompt.reshape(bp * tp, d)
    hs = x_sample.reshape(bs * ts, d)
    outs = {name: [] for name in ('sp', 'ss', 'kp', 'vp', 'ks', 'vs', 'vb')}
    for i in range(depth):
        hp, st_p, k_p, v_p, _ = _layer(i, hp, p_prompt[i].reshape(bp * tp, -1), w, BF16,
                                       batch=bp, seq=tp, sample=None)
        h0 = jnp.concatenate([state_ssm_re[i].reshape(bs, N_STATE), state_ssm_im[i].reshape(bs, N_STATE)],
                             axis=1).astype(F32)
        sample = dict(h0=h0, cache_k=cache_k, cache_v=cache_v, page_table=page_table,
                      n_slots=min(8, n_pages))
        hs, st_s, k_s, v_s, vb_s = _layer(i, hs, p_sample[i].reshape(bs * ts, -1), w, F32,
                                          batch=bs, seq=ts, sample=sample)
        outs['sp'].append(st_p)
        outs['ss'].append(st_s)
        outs['kp'].append(k_p.reshape(bp, tp, H_C, D_HEAD))
        outs['vp'].append(v_p.reshape(bp, tp, H_C, D_HEAD))
        outs['ks'].append(k_s.reshape(bs, ts, H_C, D_HEAD))
        outs['vs'].append(v_s.reshape(bs, ts, H_C, D_HEAD))
        outs['vb'].append(vb_s.reshape(bs, ts, C_B))
    st_p = jnp.stack(outs['sp'])
    st_s = jnp.stack(outs['ss'])
    state = lambda s, b, part: s[:, :, part * N_STATE:(part + 1) * N_STATE].reshape(depth, b, G_A, P_STATE)
    return (hp.reshape(bp, tp, d), hs.reshape(bs, ts, d),
            state(st_p, bp, 0), state(st_p, bp, 1), state(st_s, bs, 0), state(st_s, bs, 1),
            jnp.stack(outs['kp']), jnp.stack(outs['vp']), jnp.stack(outs['ks']), jnp.stack(outs['vs']),
            jnp.stack(outs['vb']))
```

```python
import functools

import jax
import jax.numpy as jnp
from jax import lax
from jax.experimental import pallas as pl
from jax.experimental.pallas import tpu as pltpu

F32 = jnp.float32
BF16 = jnp.bfloat16
EPS = 1e-6

C_A = 256
CPG = 16
G_A = 16
P_STATE = 64
N_STATE = G_A * P_STATE
C_B = 256
H_B = 4
HD_B = 64
CHUNK = 128
C_C = 512
D_HEAD = 64
H_C = 8
D_FF = 2816
N_EXPERTS = 8
PAGE = 128
LANES = 128

VMEM_LIMIT = 56 * 1024 * 1024


def _params(*sem):
    return pltpu.CompilerParams(dimension_semantics=sem, vmem_limit_bytes=VMEM_LIMIT)


def _dot(a, b):
    if b.dtype == BF16 or a.dtype == BF16:
        return jnp.dot(a.astype(BF16), b.astype(BF16), preferred_element_type=F32)
    return jnp.dot(a, b, preferred_element_type=F32, precision=lax.Precision.HIGHEST)


def _dot_nt(a, b):
    dn = (((1,), (1,)), ((), ()))
    if b.dtype == BF16 or a.dtype == BF16:
        return lax.dot_general(a.astype(BF16), b.astype(BF16), dn, preferred_element_type=F32)
    return lax.dot_general(a, b, dn, preferred_element_type=F32, precision=lax.Precision.HIGHEST)


def _dot_split(a, b_bf16):
    hi = a.astype(BF16)
    lo = (a - hi.astype(F32)).astype(BF16)
    return (jnp.dot(hi, b_bf16, preferred_element_type=F32)
            + jnp.dot(lo, b_bf16, preferred_element_type=F32))


def _rms(x, g):
    return x * lax.rsqrt(jnp.mean(x * x, axis=-1, keepdims=True) + EPS) * g


def _sigmoid(x):
    return 1.0 / (1.0 + jnp.exp(-x))


def _softplus(z):
    return jnp.maximum(z, 0.0) + jnp.log(1.0 + jnp.exp(-jnp.abs(z)))


def _inproj_kernel(h_ref, g_ref, w_ref, gq_ref, gk_ref, hm_ref, *out_refs, kv_copies):
    ua_ref, ub_ref, vb_ref, qa_ref, k_ref, v_ref = out_refs[:6]
    x = h_ref[...]
    xn = _rms(x, g_ref[...])
    proj = _dot(xn, w_ref[...])
    o1, o2, o3 = C_A, C_A + C_B, C_A + 2 * C_B
    o4, o5 = o3 + C_C, o3 + 2 * C_C
    ua_ref[...] = proj[:, :o1]
    ub_ref[...] = proj[:, o1:o2]
    vb_ref[...] = proj[:, o2:o3]
    q = proj[:, o3:o4]
    k = proj[:, o4:o5]
    v = proj[:, o5:]
    hm = hm_ref[...]
    qn = q * lax.rsqrt(_dot_split(q * q, hm) + EPS) * gq_ref[...]
    kn = k * lax.rsqrt(_dot_split(k * k, hm) + EPS) * gk_ref[...]
    qa_ref[...] = (qn * (D_HEAD ** -0.5)).astype(qa_ref.dtype)
    k_ref[...] = kn
    v_ref[...] = v
    if kv_copies:
        ka_ref, va_ref = out_refs[6:]
        ka_ref[...] = kn.astype(ka_ref.dtype)
        va_ref[...] = v.astype(va_ref.dtype)


def _inproj(h, g_mix, w_in, g_q, g_k, head_mean, *, tm, att_dtype, kv_copies):
    n, d = h.shape
    n_in = w_in.shape[1]
    row = lambda w: pl.BlockSpec((tm, w), lambda i: (i, 0))
    full = lambda a: pl.BlockSpec(a.shape, lambda i: (0,) * a.ndim)
    gq = jnp.tile(g_q.astype(F32), H_C)[None, :]
    gk = jnp.tile(g_k.astype(F32), H_C)[None, :]
    gm = g_mix.astype(F32)[None, :]
    out_shape = [jax.ShapeDtypeStruct((n, C_A), F32), jax.ShapeDtypeStruct((n, C_B), F32),
                 jax.ShapeDtypeStruct((n, C_B), F32), jax.ShapeDtypeStruct((n, C_C), att_dtype),
                 jax.ShapeDtypeStruct((n, C_C), F32), jax.ShapeDtypeStruct((n, C_C), F32)]
    out_specs = [row(C_A), row(C_B), row(C_B), row(C_C), row(C_C), row(C_C)]
    if kv_copies:
        out_shape += [jax.ShapeDtypeStruct((n, C_C), att_dtype)] * 2
        out_specs += [row(C_C), row(C_C)]
    return pl.pallas_call(
        functools.partial(_inproj_kernel, kv_copies=kv_copies),
        grid=(n // tm,),
        in_specs=[row(d), full(gm), full(w_in), full(gq), full(gk), full(head_mean)],
        out_specs=out_specs,
        out_shape=out_shape,
        compiler_params=_params("parallel"),
        name="inproj",
    )(h, gm, w_in, gq, gk, head_mean)


def _s5_kernel(u_ref, bbar_ref, ab_ref, cmat_ref, d_ref, wglu_ref, h0_ref, y_ref, ht_ref,
               x_s, h_s, p_s, c_s, *, seg_len, n_groups, chain):
    n = N_STATE
    s = seg_len
    nc = n // LANES
    u = u_ref[...]
    x = _dot(u, bbar_ref[...])
    for c in range(2 * nc):
        x_s[c] = x[:, c * LANES:(c + 1) * LANES]
    ar = jnp.broadcast_to(ab_ref[0:1, :], (8, n))
    ai = jnp.broadcast_to(ab_ref[1:2, :], (8, n))

    for g in range(n_groups):
        base = g * 8 * s
        if chain:
            hr0 = jnp.zeros((8, n), F32)
            hi0 = jnp.zeros((8, n), F32)
        else:
            hr0 = h0_ref[g * 8:(g + 1) * 8, :n]
            hi0 = h0_ref[g * 8:(g + 1) * 8, n:]

        def step(t, carry, base=base):
            hr, hi = carry
            rows = pl.ds(base + t, 8, stride=s)
            xr = jnp.concatenate([x_s[c, rows, :] for c in range(nc)], axis=1)
            xi = jnp.concatenate([x_s[nc + c, rows, :] for c in range(nc)], axis=1)
            nr = ar * hr - ai * hi + xr
            ni = ar * hi + ai * hr + xi
            for c in range(nc):
                h_s[c, rows, :] = nr[:, c * LANES:(c + 1) * LANES]
                h_s[nc + c, rows, :] = ni[:, c * LANES:(c + 1) * LANES]
            return nr, ni

        hr, hi = lax.fori_loop(0, s, step, (hr0, hi0))
        if not chain:
            ht_ref[g * 8:(g + 1) * 8, :n] = hr
            ht_ref[g * 8:(g + 1) * 8, n:] = hi

    if chain:
        first = pl.program_id(1) == 0

        @pl.when(first)
        def _():
            p_s[0:1, :n] = ab_ref[0:1, :]
            p_s[0:1, n:] = ab_ref[1:2, :]
            k = 1
            while k < s:
                lr = p_s[k - 1:k, :n]
                li = p_s[k - 1:k, n:]
                pr = p_s[0:k, :n]
                pi = p_s[0:k, n:]
                p_s[k:2 * k, :n] = pr * lr - pi * li
                p_s[k:2 * k, n:] = pr * li + pi * lr
                k *= 2
            c_s[...] = h0_ref[...]

        for c in range(nc):
            lanes = slice(c * LANES, (c + 1) * LANES)
            pr = p_s[:, lanes]
            pi = p_s[:, n + c * LANES:n + (c + 1) * LANES]
            cr = c_s[:, lanes]
            ci = c_s[:, n + c * LANES:n + (c + 1) * LANES]
            for j in range(8):
                rows = slice(j * s, (j + 1) * s)
                hr = h_s[c, rows, :] + pr * cr - pi * ci
                hi = h_s[nc + c, rows, :] + pr * ci + pi * cr
                h_s[c, rows, :] = hr
                h_s[nc + c, rows, :] = hi
                cr = hr[s - 1:s]
                ci = hi[s - 1:s]
            c_s[:, lanes] = cr
            c_s[:, n + c * LANES:n + (c + 1) * LANES] = ci
        ht_ref[...] = c_s[...]

    h_all = jnp.concatenate([h_s[c] for c in range(2 * nc)], axis=1)
    y = _dot(h_all, cmat_ref[...]) + d_ref[...] * u
    y = jax.nn.gelu(y)
    gl = _dot(y, wglu_ref[...])
    y_ref[...] = gl[:, :C_A] * _sigmoid(gl[:, C_A:])


def _s5(u, bbar, ab, cmat, d, wglu, h0, *, chain, rows):
    b, t, _ = u.shape
    n2 = 2 * N_STATE
    full = lambda a: pl.BlockSpec(a.shape, lambda *_: (0,) * a.ndim)
    d2 = d.astype(F32)[None, :]
    if chain:
        seg_len = rows // 8
        grid = (b, t // rows)
        u_in = u
        u_spec = pl.BlockSpec((None, rows, C_A), lambda i, j: (i, j, 0))
        y_spec = pl.BlockSpec((None, rows, C_A), lambda i, j: (i, j, 0))
        h0_in = h0[:, None, :]
        h0_spec = pl.BlockSpec((None, 1, n2), lambda i, j: (i, 0, 0))
        ht_spec = pl.BlockSpec((None, 1, n2), lambda i, j: (i, 0, 0))
        ht_shape = jax.ShapeDtypeStruct((b, 1, n2), F32)
        y_shape = jax.ShapeDtypeStruct((b, t, C_A), F32)
        n_groups = 1
        sem = ("parallel", "arbitrary")
    else:
        assert b * t == rows and b % 8 == 0
        seg_len = t
        grid = (1,)
        u_in = u.reshape(rows, C_A)
        u_spec = pl.BlockSpec((rows, C_A), lambda i: (0, 0))
        y_spec = pl.BlockSpec((rows, C_A), lambda i: (0, 0))
        h0_in = h0
        h0_spec = pl.BlockSpec((b, n2), lambda i: (0, 0))
        ht_spec = pl.BlockSpec((b, n2), lambda i: (0, 0))
        ht_shape = jax.ShapeDtypeStruct((b, n2), F32)
        y_shape = jax.ShapeDtypeStruct((rows, C_A), F32)
        n_groups = b // 8
        sem = ("arbitrary",)
    y, ht = pl.pallas_call(
        functools.partial(_s5_kernel, seg_len=seg_len, n_groups=n_groups, chain=chain),
        grid=grid,
        in_specs=[u_spec, full(bbar), full(ab), full(cmat), full(d2), full(wglu), h0_spec],
        out_specs=[y_spec, ht_spec],
        out_shape=[y_shape, ht_shape],
        scratch_shapes=[pltpu.VMEM((n2 // LANES, rows, LANES), F32), pltpu.VMEM((n2 // LANES, rows, LANES), F32),
                        pltpu.VMEM((seg_len, n2), F32), pltpu.VMEM((1, n2), F32)],
        compiler_params=_params(*sem),
        name="s5_mixer",
    )(u_in, bbar, ab, cmat, d2, wglu, h0_in)
    return y.reshape(b * t, C_A), ht.reshape(b, n2)


def _s5_params(a_re, a_im, log_dt, b_re, b_im, c_re, c_im, dtype):
    a_re = a_re.astype(F32)
    a_im = a_im.astype(F32)
    dt = jnp.exp(log_dt.astype(F32))[:, None]
    mag = jnp.exp(a_re * dt)
    ab_re = mag * jnp.cos(a_im * dt)
    ab_im = mag * jnp.sin(a_im * dt)
    den = a_re * a_re + a_im * a_im
    num_re = ab_re - 1.0
    coef_re = (num_re * a_re + ab_im * a_im) / den
    coef_im = (ab_im * a_re - num_re * a_im) / den
    b_re = b_re.astype(F32)
    b_im = b_im.astype(F32)
    bb_re = coef_re[..., None] * b_re - coef_im[..., None] * b_im
    bb_im = coef_re[..., None] * b_im + coef_im[..., None] * b_re
    eye = jnp.eye(G_A, dtype=F32)
    bd_in = lambda m: jnp.einsum('gpc,gh->gchp', m, eye).reshape(C_A, N_STATE)
    bd_out = lambda m: jnp.einsum('gcp,gh->gphc', m.astype(F32), eye).reshape(N_STATE, C_A)
    bbar = jnp.concatenate([bd_in(bb_re), bd_in(bb_im)], axis=1).astype(dtype)
    cmat = jnp.concatenate([bd_out(c_re), -bd_out(c_im)], axis=0).astype(dtype)
    ab = jnp.stack([ab_re.reshape(N_STATE), ab_im.reshape(N_STATE)])
    return bbar, ab, cmat


def _sgu_kernel(u_ref, v_ref, g_ref, w_ref, b_ref, y_ref, *vn_out, length, n_chunks):
    vn = _rms(v_ref[...], g_ref[...])
    if vn_out:
        vn_out[0][...] = vn
    head = lax.broadcasted_iota(jnp.int32, (length, C_B), 1) // HD_B
    bias = b_ref[...]
    for c in range(n_chunks):
        rows = slice(c * length, (c + 1) * length)
        vc = vn[rows]
        mixed = jnp.zeros((length, C_B), F32)
        for hh in range(H_B):
            mixed = jnp.where(head == hh, _dot(w_ref[hh], vc), mixed)
        y_ref[rows, :] = u_ref[rows, :] * (mixed + bias)


def _sgu(u, v, g_v, w_mix, bias, *, tm, want_vn):
    n = u.shape[0]
    length = w_mix.shape[1]
    row = pl.BlockSpec((tm, C_B), lambda i: (i, 0))
    full = lambda a: pl.BlockSpec(a.shape, lambda i: (0,) * a.ndim)
    g2 = g_v.astype(F32)[None, :]
    out_shape = [jax.ShapeDtypeStruct((n, C_B), F32)]
    out_specs = [row]
    if want_vn:
        out_shape.append(jax.ShapeDtypeStruct((n, C_B), F32))
        out_specs.append(row)
    res = pl.pallas_call(
        functools.partial(_sgu_kernel, length=length, n_chunks=tm // length),
        grid=(n // tm,),
        in_specs=[row, row, full(g2), full(w_mix), full(bias)],
        out_specs=out_specs,
        out_shape=out_shape,
        compiler_params=_params("parallel"),
        name="sgu_mixer",
    )(u, v, g2, w_mix, bias)
    return res if want_vn else (res[0], None)


def _attn_prompt_kernel(bias_ref, q_ref, k_ref, v_ref, tri_ref, o_ref, *, tq):
    pair = pl.program_id(1)
    qi = pl.program_id(2)
    q = q_ref[...]
    tri = tri_ref[...]
    lane_q = lax.broadcasted_iota(jnp.int32, (tq, 2 * D_HEAD), 1)
    row = lax.broadcasted_iota(jnp.int32, (tq, tq), 0)
    col = lax.broadcasted_iota(jnp.int32, (tq, tq), 1)
    causal = col < row
    head_q = lane_q // D_HEAD
    qms = [jnp.where(head_q == hh, q, jnp.zeros_like(q)) for hh in range(2)]
    biases = [bias_ref[2 * pair + hh] for hh in range(2)]

    def block(kb, state, masked):
        start = pl.multiple_of(kb * tq, tq)
        kblk = k_ref[pl.ds(start, tq), :]
        vblk = v_ref[pl.ds(start, tq), :]
        new = []
        for hh in range(2):
            carry, acc = state[2 * hh], state[2 * hh + 1]
            z = _dot_nt(qms[hh], kblk) + biases[hh]
            sp = _softplus(z)
            if masked:
                sp = jnp.where(causal, sp, 0.0)
            later = _dot(sp.astype(BF16), tri)
            a = jnp.exp(z - sp - later - carry)
            if masked:
                a = jnp.where(causal, a, 0.0)
            new.append(carry + jnp.sum(sp, axis=-1, keepdims=True))
            new.append(acc + _dot(a.astype(BF16), vblk))
        return tuple(new)

    zero_c = jnp.zeros((tq, 1), F32)
    zero_a = jnp.zeros((tq, 2 * D_HEAD), F32)
    state = block(qi, (zero_c, zero_a, zero_c, zero_a), True)
    state = lax.fori_loop(0, qi // 2, lambda i, st: block(qi - 2 - 2 * i, block(qi - 1 - 2 * i, st, False), False),
                          state)
    state = lax.fori_loop(0, qi % 2, lambda i, st: block(0, st, False), state)
    o_ref[...] = jnp.where(head_q == 0, state[1], state[3])


def _attn_prompt(q, k, v, sb_bias, *, tq):
    b, t, _ = q.shape
    tri = (jnp.arange(tq)[:, None] > jnp.arange(tq)[None, :]).astype(BF16)
    lanes = 2 * D_HEAD
    return pl.pallas_call(
        functools.partial(_attn_prompt_kernel, tq=tq),
        grid_spec=pltpu.PrefetchScalarGridSpec(
            num_scalar_prefetch=1,
            grid=(b, H_C // 2, t // tq),
            in_specs=[pl.BlockSpec((None, tq, lanes), lambda i, p, j, *_: (i, j, p)),
                      pl.BlockSpec((None, t, lanes), lambda i, p, j, *_: (i, 0, p)),
                      pl.BlockSpec((None, t, lanes), lambda i, p, j, *_: (i, 0, p)),
                      pl.BlockSpec((tq, tq), lambda i, p, j, *_: (0, 0))],
            out_specs=pl.BlockSpec((None, tq, lanes), lambda i, p, j, *_: (i, j, p)),
        ),
        out_shape=jax.ShapeDtypeStruct((b, t, C_C), F32),
        compiler_params=_params("parallel", "parallel", "arbitrary"),
        name="attn_prompt",
    )(sb_bias.astype(F32), q, k, v, tri)


def _attn_sample_kernel(pt_ref, q_ref, kn_ref, vn_ref, bias_ref, tri_ref, *rest, n_slots):
    k_refs = rest[:n_slots]
    v_refs = rest[n_slots:2 * n_slots]
    o_ref = rest[2 * n_slots]
    acc_s, carry_s = rest[2 * n_slots + 1:]
    step = pl.program_id(1)
    nq = q_ref.shape[0]
    rows = H_C * nq
    q = q_ref[...]
    row_head = lax.broadcasted_iota(jnp.int32, (rows, C_C), 0) // nq
    lane_head = lax.broadcasted_iota(jnp.int32, (rows, C_C), 1) // D_HEAD
    q_bd = jnp.where(row_head == lane_head, jnp.concatenate([q] * H_C, axis=0), jnp.zeros((), q.dtype))
    bias = bias_ref[...]
    tri = tri_ref[...]

    def weights(z, carry, keep):
        sp = _softplus(z)
        if keep is not None:
            sp = jnp.where(keep, sp, 0.0)
        later = _dot_split(sp, tri)
        a = jnp.exp(z - sp - later - carry)
        if keep is not None:
            a = jnp.where(keep, a, 0.0)
        return a.astype(BF16), carry + jnp.sum(sp, axis=-1, keepdims=True)

    @pl.when(step == 0)
    def _():
        z = _dot_nt(q_bd, kn_ref[...]) + bias
        pos = lax.broadcasted_iota(jnp.int32, (rows, PAGE), 1)
        qpos = lax.broadcasted_iota(jnp.int32, (rows, PAGE), 0) % nq
        a, carry = weights(z, jnp.zeros((rows, 1), F32), pos < qpos)
        acc_s[...] = _dot(a, vn_ref[...])
        carry_s[...] = carry

    acc = acc_s[...]
    carry = carry_s[...]
    for r in range(n_slots):
        kt = k_refs[r][...].reshape(C_C, PAGE).astype(BF16)
        vt = v_refs[r][...].reshape(C_C, PAGE).astype(BF16)
        a, carry = weights(_dot(q_bd, kt) + bias, carry, None)
        acc = acc + _dot_nt(a, vt)
    acc_s[...] = acc
    carry_s[...] = carry

    @pl.when(step == pl.num_programs(1) - 1)
    def _():
        for hh in range(H_C):
            lanes = slice(hh * D_HEAD, (hh + 1) * D_HEAD)
            o_ref[:, lanes] = acc[hh * nq:(hh + 1) * nq, lanes]


def _attn_sample(q, k_new, v_new, cache_kt, cache_vt, page_table, sb_bias, layer, *, n_slots):
    n_seq, n_pages = page_table.shape
    nq = q.shape[1]
    rows = H_C * nq
    steps = n_pages // n_slots
    bias = jnp.broadcast_to(jnp.repeat(sb_bias.astype(F32), nq)[:, None], (rows, PAGE))
    tri = (jnp.arange(PAGE)[:, None] > jnp.arange(PAGE)[None, :]).astype(BF16)

    def page_spec(r):
        def imap(b, s, pt):
            return (layer, pt[b * n_pages + (n_pages - 1 - (s * n_slots + r))], 0, 0, 0)
        return pl.BlockSpec((None, None, H_C, D_HEAD, PAGE), imap)

    seq = lambda a: pl.BlockSpec((None,) + a.shape[1:], lambda b, s, pt: (b,) + (0,) * (a.ndim - 1))
    const = lambda a: pl.BlockSpec(a.shape, lambda b, s, pt: (0,) * a.ndim)
    return pl.pallas_call(
        functools.partial(_attn_sample_kernel, n_slots=n_slots),
        grid_spec=pltpu.PrefetchScalarGridSpec(
            num_scalar_prefetch=1,
            grid=(n_seq, steps),
            in_specs=[seq(q), seq(k_new), seq(v_new), const(bias), const(tri)]
                     + [page_spec(r) for r in range(n_slots)]
                     + [page_spec(r) for r in range(n_slots)],
            out_specs=seq(q),
            scratch_shapes=[pltpu.VMEM((rows, C_C), F32), pltpu.VMEM((rows, 1), F32)],
        ),
        out_shape=jax.ShapeDtypeStruct(q.shape, F32),
        compiler_params=_params("parallel", "arbitrary"),
        name="attn_sample",
    )(page_table.reshape(-1), q, k_new, v_new, bias, tri,
      *([cache_kt] * n_slots), *([cache_vt] * n_slots))


def _outproj_kernel(ya_ref, yb_ref, yc_ref, h_ref, g_ref, w_ref, *rest, router):
    g = g_ref[...]
    mix = jnp.concatenate([_rms(ya_ref[...], g[:, :C_A]),
                           _rms(yb_ref[...], g[:, C_A:C_A + C_B]),
                           _rms(yc_ref[...], g[:, C_A + C_B:])], axis=-1)
    h1 = h_ref[...] + _dot(mix, w_ref[...])
    if not router:
        rest[0][...] = h1
        return
    gf_ref, wr_ref, br_ref, o_ref, gates_ref = rest
    o_ref[...] = h1
    xn = _rms(h1, gf_ref[...])
    logits = jnp.dot(xn, wr_ref[...], preferred_element_type=F32,
                     precision=lax.Precision.HIGHEST) + br_ref[...]
    e_idx = lax.broadcasted_iota(jnp.int32, logits.shape, 1).astype(F32)
    none = float(N_EXPERTS)
    v1 = jnp.max(logits, axis=-1, keepdims=True)
    i1 = jnp.min(jnp.where(logits == v1, e_idx, none), axis=-1, keepdims=True)
    rest_l = jnp.where(e_idx == i1, -jnp.inf, logits)
    v2 = jnp.max(rest_l, axis=-1, keepdims=True)
    i2 = jnp.min(jnp.where(rest_l == v2, e_idx, none), axis=-1, keepdims=True)
    e2 = jnp.exp(v2 - v1)
    den = 1.0 + e2
    gates_ref[...] = jnp.where(e_idx == i1, 1.0 / den, 0.0) + jnp.where(e_idx == i2, e2 / den, 0.0)


def _outproj(ya, yb, yc, h, g_grp, w_out, *, tm, router=None):
    n, d = h.shape
    row = lambda w: pl.BlockSpec((tm, w), lambda i: (i, 0))
    full = lambda a: pl.BlockSpec(a.shape, lambda i: (0,) * a.ndim)
    g2 = g_grp.astype(F32)[None, :]
    args = [ya, yb, yc, h, g2, w_out]
    in_specs = [row(C_A), row(C_B), row(C_C), row(d), full(g2), full(w_out)]
    out_shape = [jax.ShapeDtypeStruct((n, d), F32)]
    out_specs = [row(d)]
    if router is not None:
        g_ffn, w_router, b_router = router
        extra = [g_ffn.astype(F32)[None, :], w_router.astype(F32), b_router.astype(F32)[None, :]]
        args += extra
        in_specs += [full(a) for a in extra]
        out_shape.append(jax.ShapeDtypeStruct((n, N_EXPERTS), F32))
        out_specs.append(row(N_EXPERTS))
    res = pl.pallas_call(
        functools.partial(_outproj_kernel, router=router is not None),
        grid=(n // tm,),
        in_specs=in_specs,
        out_specs=out_specs,
        out_shape=out_shape,
        compiler_params=_params("parallel"),
        name="outproj",
    )(*args)
    return res if router is not None else (res[0], None)


def _ffn_kernel(h_ref, g_ref, wg_ref, wu_ref, wd_ref, *rest, gated):
    if gated:
        gates_ref, o_ref, xn_s, acc_s, exp_s = rest
    else:
        o_ref, xn_s, acc_s = rest
        exp_s = acc_s
    e = pl.program_id(1)
    j = pl.program_id(2)
    nj = pl.num_programs(2)

    @pl.when((e == 0) & (j == 0))
    def _():
        h = h_ref[...]
        xn_s[...] = _rms(h, g_ref[...]).astype(xn_s.dtype)
        acc_s[...] = h

    xn = xn_s[...]
    gate = _dot(xn, wg_ref[...])
    up = _dot(xn, wu_ref[...])
    act = gate * _sigmoid(gate) * up
    part = _dot(act, wd_ref[...])
    if gated:
        @pl.when(j == 0)
        def _():
            exp_s[...] = part

        @pl.when(j > 0)
        def _():
            exp_s[...] += part

        @pl.when(j == nj - 1)
        def _():
            gates = gates_ref[...]
            lane = lax.broadcasted_iota(jnp.int32, gates.shape, 1)
            ge = jnp.sum(jnp.where(lane == e, gates, 0.0), axis=-1, keepdims=True)
            acc_s[...] += ge * exp_s[...]
    else:
        acc_s[...] += part

    @pl.when((e == pl.num_programs(1) - 1) & (j == nj - 1))
    def _():
        o_ref[...] = acc_s[...]


def _ffn(h, g_ffn, w_up, w_down, gates=None, *, tm, tf):
    n, d = h.shape
    n_e = w_up.shape[0]
    nj = D_FF // tf
    gated = gates is not None
    g2 = g_ffn.astype(F32)[None, :]
    row = pl.BlockSpec((tm, d), lambda i, e, j: (i, 0))
    in_specs = [row, pl.BlockSpec(g2.shape, lambda i, e, j: (0, 0)),
                pl.BlockSpec((None, d, tf), lambda i, e, j: (e, 0, j)),
                pl.BlockSpec((None, d, tf), lambda i, e, j: (e, 0, j + nj)),
                pl.BlockSpec((None, tf, d), lambda i, e, j: (e, j, 0))]
    args = [h, g2, w_up, w_up, w_down]
    cdt = w_up.dtype
    scratch = [pltpu.VMEM((tm, d), cdt), pltpu.VMEM((tm, d), F32)]
    if gated:
        in_specs.append(pl.BlockSpec((tm, N_EXPERTS), lambda i, e, j: (i, 0)))
        args.append(gates)
        scratch.append(pltpu.VMEM((tm, d), F32))
    return pl.pallas_call(
        functools.partial(_ffn_kernel, gated=gated),
        grid=(n // tm, n_e, nj),
        in_specs=in_specs,
        out_specs=row,
        out_shape=jax.ShapeDtypeStruct((n, d), F32),
        scratch_shapes=scratch,
        compiler_params=_params("parallel", "arbitrary", "arbitrary"),
        name="ffn",
    )(*args)


def _ple_kernel(h_ref, p_ref, g_ref, wg_ref, wp_ref, o_ref):
    h = h_ref[...]
    gate = _sigmoid(_dot(_rms(h, g_ref[...]), wg_ref[...]))
    o_ref[...] = h + gate * _dot(p_ref[...], wp_ref[...])


def _ple(h, p, g_ple, w_gate, w_ple, *, tm):
    n, d = h.shape
    g2 = g_ple.astype(F32)[None, :]
    row = lambda w: pl.BlockSpec((tm, w), lambda i: (i, 0))
    full = lambda a: pl.BlockSpec(a.shape, lambda i: (0,) * a.ndim)
    return pl.pallas_call(
        _ple_kernel,
        grid=(n // tm,),
        in_specs=[row(d), row(p.shape[1]), full(g2), full(w_gate), full(w_ple)],
        out_specs=row(d),
        out_shape=jax.ShapeDtypeStruct((n, d), F32),
        compiler_params=_params("parallel"),
        name="ple",
    )(h, p, g2, w_gate, w_ple)


def _tile(n, want):
    t = min(n, want)
    while n % t:
        t //= 2
    return t


def _layer(i, h, p, w, dtype, *, batch, seq, sample):
    n = batch * seq
    cast = lambda a: a.astype(dtype)
    head_mean = (jnp.arange(C_C)[:, None] // D_HEAD == jnp.arange(C_C)[None, :] // D_HEAD)
    head_mean = (head_mean.astype(F32) / D_HEAD).astype(BF16)
    ua, ub, vb, qa, k, v, *kv_att = _inproj(
        h, w['g_mix'][i], cast(w['w_in'][i]), w['g_q'][i], w['g_k'][i], head_mean,
        tm=_tile(n, 512), att_dtype=BF16, kv_copies=sample is None)

    bbar, ab, cmat = _s5_params(w['ssm_a_re'][i], w['ssm_a_im'][i], w['ssm_log_dt'][i],
                                w['ssm_b_re'][i], w['ssm_b_im'][i], w['ssm_c_re'][i], w['ssm_c_im'][i], dtype)
    wglu = cast(w['w_ssm_glu'][i])
    w_s = w['sgu_w_s'][i]
    b_s = w['sgu_b_s'][i]
    if sample is None:
        h0 = jnp.zeros((batch, 2 * N_STATE), F32)
        ya, ht = _s5(ua.reshape(batch, seq, C_A), bbar, ab, cmat, w['ssm_d'][i], wglu, h0,
                     chain=True, rows=_tile(seq, 512))
        length = min(seq, CHUNK)
        w_mix = cast(w_s[:, :length, :length] * jnp.tril(jnp.ones((length, length), w_s.dtype)))
        bias = jnp.repeat(jnp.transpose(b_s[:, :length]), HD_B, axis=1).astype(F32)
        yb, vn = _sgu(ub, vb, w['sgu_g_v'][i], w_mix, bias, tm=_tile(n, 512), want_vn=False)
        yc = _attn_prompt(qa.reshape(batch, seq, C_C), kv_att[0].reshape(batch, seq, C_C),
                          kv_att[1].reshape(batch, seq, C_C), w['sb_bias'][i], tq=_tile(seq, 256))
        yc = yc.reshape(n, C_C)
    else:
        ya, ht = _s5(ua.reshape(batch, seq, C_A), bbar, ab, cmat, w['ssm_d'][i], wglu, sample['h0'],
                     chain=False, rows=n)
        length = min(seq, CHUNK)
        blk = w_s[:, :length, :length] * jnp.tril(jnp.ones((length, length), w_s.dtype))
        w_mix = cast(jnp.einsum('hts,bc->hbtcs', blk, jnp.eye(batch, dtype=blk.dtype)).reshape(H_B, n, n))
        bias = jnp.tile(jnp.repeat(jnp.transpose(b_s[:, :length]), HD_B, axis=1), (batch, 1)).astype(F32)
        yb, vn = _sgu(ub, vb, w['sgu_g_v'][i], w_mix, bias, tm=n, want_vn=True)
        new = lambda a: jnp.pad(a.astype(BF16).reshape(batch, seq, C_C), ((0, 0), (0, PAGE - seq), (0, 0)))
        yc = _attn_sample(qa.reshape(batch, seq, C_C), new(k), new(v), sample['cache_kt'], sample['cache_vt'],
                          sample['page_table'], w['sb_bias'][i], i, n_slots=sample['n_slots'])
        yc = yc.reshape(n, C_C)

    moe = i % 2 == 1
    jj = i // 2
    router = (w['g_ffn'][i], w['w_router'][jj], w['b_router'][jj]) if moe else None
    h1, gates = _outproj(ya, yb, yc, h, w['g_grp'][i], cast(w['w_out'][i]), tm=_tile(n, 512), router=router)
    depth = w['w_in'].shape[0]
    last_router = depth - 1 if depth % 2 == 0 else depth - 2
    cast_ffn = (lambda a: a.astype(BF16)) if i >= last_router else cast
    cast_ple = cast_ffn
    if moe:
        h2 = _ffn(h1, w['g_ffn'][i], cast_ffn(w['w_exp_up'][jj]), cast_ffn(w['w_exp_down'][jj]), gates,
                  tm=_tile(n, 1024), tf=256)
    else:
        h2 = _ffn(h1, w['g_ffn'][i], cast_ffn(w['w_ffn_up'][jj])[None], cast_ffn(w['w_ffn_down'][jj])[None],
                  tm=_tile(n, 1024), tf=256)
    h3 = _ple(h2, p, w['g_ple'][i], cast_ple(w['w_ple_gate'][i]), cast_ple(w['w_ple'][i]), tm=_tile(n, 512))
    return h3, ht, k, v, vn


def kernel(x_prompt, x_sample, p_prompt, p_sample, state_ssm_re, state_ssm_im, cache_k, cache_v, page_table, g_mix, w_in, ssm_a_re, ssm_a_im, ssm_log_dt, ssm_b_re, ssm_b_im, ssm_c_re, ssm_c_im, ssm_d, w_ssm_glu, sgu_g_v, sgu_w_s, sgu_b_s, g_q, g_k, sb_bias, g_grp, w_out, g_ffn, w_ffn_up, w_ffn_down, w_router, b_router, w_exp_up, w_exp_down, w_ple, g_ple, w_ple_gate):
    w = dict(g_mix=g_mix, w_in=w_in, ssm_a_re=ssm_a_re, ssm_a_im=ssm_a_im, ssm_log_dt=ssm_log_dt,
             ssm_b_re=ssm_b_re, ssm_b_im=ssm_b_im, ssm_c_re=ssm_c_re, ssm_c_im=ssm_c_im, ssm_d=ssm_d,
             w_ssm_glu=w_ssm_glu, sgu_g_v=sgu_g_v, sgu_w_s=sgu_w_s, sgu_b_s=sgu_b_s, g_q=g_q, g_k=g_k,
             sb_bias=sb_bias, g_grp=g_grp, w_out=w_out, g_ffn=g_ffn, w_ffn_up=w_ffn_up, w_ffn_down=w_ffn_down,
             w_router=w_router, b_router=b_router, w_exp_up=w_exp_up, w_exp_down=w_exp_down,
             w_ple=w_ple, g_ple=g_ple, w_ple_gate=w_ple_gate)
    depth = w_in.shape[0]
    bp, tp, d = x_prompt.shape
    bs, ts, _ = x_sample.shape
    n_pages = page_table.shape[1]
    hp = x_prompt.reshape(bp * tp, d)
    hs = x_sample.reshape(bs * ts, d)
    outs = {name: [] for name in ('sp', 'ss', 'kp', 'vp', 'ks', 'vs', 'vb')}
    cache_kt = jnp.transpose(cache_k, (0, 1, 3, 4, 2))
    cache_vt = jnp.transpose(cache_v, (0, 1, 3, 4, 2))
    for i in range(depth):
        hp, st_p, k_p, v_p, _ = _layer(i, hp, p_prompt[i].reshape(bp * tp, -1), w, BF16,
                                       batch=bp, seq=tp, sample=None)
        h0 = jnp.concatenate([state_ssm_re[i].reshape(bs, N_STATE), state_ssm_im[i].reshape(bs, N_STATE)],
                             axis=1).astype(F32)
        sample = dict(h0=h0, cache_kt=cache_kt, cache_vt=cache_vt, page_table=page_table,
                      n_slots=min(16, n_pages))
        hs, st_s, k_s, v_s, vb_s = _layer(i, hs, p_sample[i].reshape(bs * ts, -1), w, F32,
                                          batch=bs, seq=ts, sample=sample)
        outs['sp'].append(st_p)
        outs['ss'].append(st_s)
        outs['kp'].append(k_p.reshape(bp, tp, H_C, D_HEAD))
        outs['vp'].append(v_p.reshape(bp, tp, H_C, D_HEAD))
        outs['ks'].append(k_s.reshape(bs, ts, H_C, D_HEAD))
        outs['vs'].append(v_s.reshape(bs, ts, H_C, D_HEAD))
        outs['vb'].append(vb_s.reshape(bs, ts, C_B))
    st_p = jnp.stack(outs['sp'])
    st_s = jnp.stack(outs['ss'])
    state = lambda s, b, part: s[:, :, part * N_STATE:(part + 1) * N_STATE].reshape(depth, b, G_A, P_STATE)
    return (hp.reshape(bp, tp, d), hs.reshape(bs, ts, d),
            state(st_p, bp, 0), state(st_p, bp, 1), state(st_s, bs, 0), state(st_s, bs, 1),
            jnp.stack(outs['kp']), jnp.stack(outs['vp']), jnp.stack(outs['ks']), jnp.stack(outs['vs']),
            jnp.stack(outs['vb']))
```
